```python
import math
import jax, jax.numpy as jnp
from jax import lax
import numpy as np

D_MODEL = 1024
BATCH = 8
SEQ = 2048
DEPTH = 2

MIX_WIDTH = D_MODEL
HEAD_DIM = 64
ATTN_WIDTH = D_MODEL // 2
N_Q_HEADS = ATTN_WIDTH // HEAD_DIM
N_KV_HEADS = N_Q_HEADS // 4
KV_WIDTH = N_KV_HEADS * HEAD_DIM
WINDOW = 128
BLOCK = 128
ROPE_THETA = 500000.0
ROPE_DIM = HEAD_DIM // 4
CONV_WIDTH = D_MODEL // 4
CONV_K = 3
POOL_WIDTH = D_MODEL // 4
POOL_WINDOWS = (2, 4, 8, 16)
POOL_GROUP = POOL_WIDTH // len(POOL_WINDOWS)
IN_WIDTH = ATTN_WIDTH + 2 * KV_WIDTH + 3 * CONV_WIDTH + POOL_WIDTH
IN_SPLITS = (ATTN_WIDTH,
             ATTN_WIDTH + KV_WIDTH,
             ATTN_WIDTH + 2 * KV_WIDTH,
             ATTN_WIDTH + 2 * KV_WIDTH + CONV_WIDTH,
             ATTN_WIDTH + 2 * KV_WIDTH + 2 * CONV_WIDTH,
             ATTN_WIDTH + 2 * KV_WIDTH + 3 * CONV_WIDTH)
PEER_HEADS = 8
PEER_NKEYS = 128
PEER_EXPERTS = PEER_NKEYS * PEER_NKEYS
PEER_QDIM = 256
PEER_TOPK = 16
PEER_CHUNK = 128
EPS = 1e-6
MASK_VALUE = -1e30

kernel_name = "hybrid_conv_swa_pool_peer_encoder"


def rms_norm(t, g):
    tf = t.astype(jnp.float32)
    y = tf * lax.rsqrt(jnp.mean(tf * tf, axis=-1, keepdims=True) + EPS)
    return (y * g.astype(jnp.float32)).astype(t.dtype)


def rope_tables(positions):
    inv = ROPE_THETA ** (-jnp.arange(0, ROPE_DIM, 2, dtype=jnp.float32) / ROPE_DIM)
    ang = positions.astype(jnp.float32)[..., None] * inv
    return jnp.cos(ang)[:, :, None, :], jnp.sin(ang)[:, :, None, :]


def partial_rope(t, cos, sin):
    half = ROPE_DIM // 2
    t1 = t[..., :half].astype(jnp.float32)
    t2 = t[..., half:ROPE_DIM].astype(jnp.float32)
    rot = jnp.concatenate([t1 * cos - t2 * sin, t2 * cos + t1 * sin], axis=-1).astype(t.dtype)
    return jnp.concatenate([rot, t[..., ROPE_DIM:]], axis=-1)


def windowed_gqa(q, k, v, sink):
    b, s = q.shape[0], q.shape[1]
    nb = s // BLOCK
    grp = N_Q_HEADS // N_KV_HEADS
    qb = q.reshape(b, nb, BLOCK, N_KV_HEADS, grp, HEAD_DIM)

    def bands(t):
        tp = jnp.pad(t, ((0, 0), (BLOCK, BLOCK), (0, 0), (0, 0)))
        tp = tp.reshape(b, nb + 2, BLOCK, N_KV_HEADS, HEAD_DIM)
        return jnp.concatenate([tp[:, :-2], tp[:, 1:-1], tp[:, 2:]], axis=2)

    kb, vb = bands(k), bands(v)
    scores = jnp.einsum('bnqhgd,bnkhd->bnhgqk', qb, kb).astype(jnp.float32) * (HEAD_DIM ** -0.5)
    qpos = jnp.arange(s).reshape(nb, BLOCK, 1)
    kpos = (jnp.arange(nb)[:, None, None] - 1) * BLOCK + jnp.arange(3 * BLOCK)[None, None, :]
    valid = (jnp.abs(qpos - kpos) <= WINDOW) & (kpos >= 0) & (kpos < s)
    scores = jnp.where(valid[None, :, None, None], scores, MASK_VALUE)
    sink_col = jnp.broadcast_to(
        sink.astype(jnp.float32).reshape(1, 1, N_KV_HEADS, grp, 1, 1), scores.shape[:-1] + (1,))
    probs = jax.nn.softmax(jnp.concatenate([scores, sink_col], axis=-1), axis=-1)[..., :-1]
    out = jnp.einsum('bnhgqk,bnkhd->bnqhgd', probs.astype(v.dtype), vb)
    return out.reshape(b, s, N_Q_HEADS * HEAD_DIM)


def short_conv(u, w):
    up = jnp.pad(u, ((0, 0), (1, 1), (0, 0)))
    return w[0] * up[:, :-2] + w[1] * up[:, 1:-1] + w[2] * up[:, 2:]


def multiscale_pool(u):
    s = u.shape[1]
    t = jnp.arange(s)
    uf = u.astype(jnp.float32)
    cs = jnp.pad(jnp.cumsum(uf, axis=1), ((0, 0), (1, 0), (0, 0)))
    outs = []
    for gi, w in enumerate(POOL_WINDOWS):
        lo = jnp.clip(t - w // 2, 0, s)
        hi = jnp.clip(t + w // 2, 0, s)
        csg = cs[..., gi * POOL_GROUP:(gi + 1) * POOL_GROUP]
        cnt = (hi - lo).astype(jnp.float32)[None, :, None]
        mean = (jnp.take(csg, hi, axis=1) - jnp.take(csg, lo, axis=1)) / cnt
        outs.append(mean - uf[..., gi * POOL_GROUP:(gi + 1) * POOL_GROUP])
    return jnp.stack(outs, axis=2).astype(u.dtype)


def peer_ffn(h, w_q, sub_keys, expert_u, expert_v):
    b, s, d = h.shape
    n_tok = b * s
    ht = h.reshape(n_tok, d)
    q = (ht @ w_q).reshape(n_tok, PEER_HEADS, PEER_QDIM)
    half = PEER_QDIM // 2
    s1 = jnp.einsum('thd,kd->thk', q[..., :half], sub_keys[0]).astype(jnp.float32)
    s2 = jnp.einsum('thd,kd->thk', q[..., half:], sub_keys[1]).astype(jnp.float32)
    v1, i1 = lax.top_k(s1, PEER_TOPK)
    v2, i2 = lax.top_k(s2, PEER_TOPK)
    cand = (v1[..., :, None] + v2[..., None, :]).reshape(n_tok, PEER_HEADS, PEER_TOPK * PEER_TOPK)
    cidx = (i1[..., :, None] * PEER_NKEYS + i2[..., None, :]).reshape(n_tok, PEER_HEADS, PEER_TOPK * PEER_TOPK)
    top, pos = lax.top_k(cand, PEER_TOPK)
    eidx = jnp.take_along_axis(cidx, pos, axis=-1)
    gate = jax.nn.softmax(top, axis=-1).astype(h.dtype)
    n_chunks = n_tok // PEER_CHUNK

    def chunk(args):
        hc, ec, gc = args
        u = expert_u[ec]
        act = jax.nn.gelu(jnp.einsum('td,thkd->thk', hc, u), approximate=False)
        return jnp.einsum('thk,thkd->td', gc * act, expert_v[ec])

    y = lax.map(chunk, (ht.reshape(n_chunks, PEER_CHUNK, d),
                        eidx.reshape(n_chunks, PEER_CHUNK, PEER_HEADS, PEER_TOPK),
                        gate.reshape(n_chunks, PEER_CHUNK, PEER_HEADS, PEER_TOPK)))
    return y.reshape(b, s, d)


def setup_inputs(seed: int = 0) -> dict:
    key = jax.random.key(seed)
    ks = jax.random.split(key, 20)
    L, D = DEPTH, D_MODEL

    def nrm(k, shape, scale):
        return jax.random.normal(k, shape, jnp.float32) * scale

    return {
        "x": nrm(ks[0], (BATCH, SEQ, D), 1.0),
        "c": nrm(ks[1], (BATCH, D), 1.0),
        "positions": jnp.arange(SEQ, dtype=jnp.int32)[None, :]
                     + jax.random.randint(ks[2], (BATCH, 1), 0, 1024, dtype=jnp.int32),
        "norm1_g": 1.0 + nrm(ks[3], (L, D), 0.02),
        "norm2_g": 1.0 + nrm(ks[4], (L, D), 0.02),
        "w_ada": nrm(ks[5], (L, D, 6 * D), 0.5 * D ** -0.5),
        "b_ada": nrm(ks[6], (L, 6 * D), 0.02),
        "w_in": nrm(ks[7], (L, D, IN_WIDTH), D ** -0.5),
        "q_norm_g": 1.0 + nrm(ks[8], (L, HEAD_DIM), 0.02),
        "k_norm_g": 1.0 + nrm(ks[9], (L, HEAD_DIM), 0.02),
        "attn_sink": nrm(ks[10], (L, N_Q_HEADS), 0.5),
        "conv_w": nrm(ks[11], (L, CONV_K, CONV_WIDTH), CONV_K ** -0.5),
        "pool_w": nrm(ks[12], (L, len(POOL_WINDOWS), POOL_GROUP, POOL_GROUP), POOL_GROUP ** -0.5),
        "pool_scale": 1.0 + nrm(ks[13], (L, POOL_WIDTH), 0.1),
        "w_out": nrm(ks[14], (L, MIX_WIDTH, D), MIX_WIDTH ** -0.5),
        "peer_wq": nrm(ks[15], (L, D, PEER_HEADS * PEER_QDIM), D ** -0.5),
        "peer_keys": nrm(ks[16], (L, 2, PEER_NKEYS, PEER_QDIM // 2), (PEER_QDIM // 2) ** -0.5),
        "peer_u": nrm(ks[17], (L, PEER_EXPERTS, D), D ** -0.5),
        "peer_v": nrm(ks[18], (L, PEER_EXPERTS, D), PEER_HEADS ** -0.5),
    }


def reference(x, c, positions, norm1_g, norm2_g, w_ada, b_ada, w_in, q_norm_g, k_norm_g,
              attn_sink, conv_w, pool_w, pool_scale, w_out, peer_wq, peer_keys, peer_u, peer_v):
    b, s, d = x.shape
    cos, sin = rope_tables(positions)
    c_act = jax.nn.silu(c)
    for l in range(DEPTH):
        mod = (c_act @ w_ada[l] + b_ada[l])[:, None, :]
        shift1, scale1, gate1, shift2, scale2, gate2 = jnp.split(mod, 6, axis=-1)

        h = rms_norm(x, norm1_g[l]) * (1.0 + scale1) + shift1
        z = h @ w_in[l]
        q, k, v, conv_x, conv_b, conv_c, pool_x = jnp.split(z, IN_SPLITS, axis=-1)

        q = q.reshape(b, s, N_Q_HEADS, HEAD_DIM)
        k = k.reshape(b, s, N_KV_HEADS, HEAD_DIM)
        v = v.reshape(b, s, N_KV_HEADS, HEAD_DIM)
        q = partial_rope(rms_norm(q, q_norm_g[l]), cos, sin)
        k = partial_rope(rms_norm(k, k_norm_g[l]), cos, sin)
        attn_o = windowed_gqa(q, k, v, attn_sink[l])

        conv_o = conv_b * short_conv(conv_c * conv_x, conv_w[l])

        pooled = multiscale_pool(pool_x)
        pool_o = jnp.einsum('bsgc,gce->bsge', pooled, pool_w[l]).reshape(b, s, POOL_WIDTH) * pool_scale[l]

        mix = jnp.concatenate([attn_o, conv_o, pool_o], axis=-1) @ w_out[l]
        x = x + gate1 * mix

        h2 = rms_norm(x, norm2_g[l]) * (1.0 + scale2) + shift2
        x = x + gate2 * peer_ffn(h2, peer_wq[l], peer_keys[l], peer_u[l], peer_v[l])
    return x
```

```python
import functools
import math

import jax
import jax.numpy as jnp
from jax import lax
from jax.experimental import pallas as pl
from jax.experimental.pallas import tpu as pltpu

F32 = jnp.float32
BF16 = jnp.bfloat16

HEAD_DIM = 64
N_Q_HEADS = 8
N_KV_HEADS = 2
ATTN_WIDTH = N_Q_HEADS * HEAD_DIM
KV_WIDTH = N_KV_HEADS * HEAD_DIM
BLOCK = 128
ROPE_THETA = 500000.0
ROPE_DIM = 16
CONV_WIDTH = 256
POOL_WIDTH = 256
POOL_GROUP = 64
POOL_HALO = 8
PEER_HEADS = 8
PEER_NKEYS = 128
PEER_QDIM = 256
PEER_TOPK = 16
EPS = 1e-6
MASK_VALUE = -1e30
INV_SQRT2 = 1.0 / math.sqrt(2.0)

LANES = 128
VMEM_LIMIT = 48 * 1024 * 1024

_NT = (((1,), (1,)), ((), ()))


def _params(sem):
    return pltpu.CompilerParams(dimension_semantics=sem, vmem_limit_bytes=VMEM_LIMIT)


def _ada_kernel(c_ref, w_ref, b_ref, o_ref):
    c = c_ref[...]
    c_act = c * (1.0 / (1.0 + jnp.exp(-c)))
    o_ref[...] = jnp.dot(c_act.astype(BF16), w_ref[...].astype(BF16),
                         preferred_element_type=F32) + b_ref[...]


def _ada(c, w_ada, b_ada):
    n_layers, d, d6 = w_ada.shape
    b = c.shape[0]
    return pl.pallas_call(
        _ada_kernel,
        grid=(n_layers, d6 // d),
        in_specs=[pl.BlockSpec((b, d), lambda l, j: (0, 0)),
                  pl.BlockSpec((None, d, d), lambda l, j: (l, 0, j)),
                  pl.BlockSpec((None, 1, d), lambda l, j: (l, 0, j))],
        out_specs=pl.BlockSpec((None, b, d), lambda l, j: (l, 0, j)),
        out_shape=jax.ShapeDtypeStruct((n_layers, b, d6), F32),
        compiler_params=_params(("arbitrary", "arbitrary")),
        name="ada",
    )(c, w_ada, b_ada.reshape(n_layers, 1, d6))


def _in_kernel(x_ref, mod_ref, g1_ref, win_ref, qg_ref, kg_ref, gavg_ref, pos_ref,
               inv_ref, sgn_ref, q_ref, kvx_ref, cp_ref, z_scr):
    x = x_ref[...]
    ms = jnp.mean(x * x, axis=-1, keepdims=True)
    y = x * lax.rsqrt(ms + EPS) * g1_ref[...]
    h = y * (1.0 + mod_ref[1:2, :]) + mod_ref[0:1, :]
    z_scr[...] = jnp.dot(h.astype(BF16), win_ref[...], preferred_element_type=F32)

    tm = x.shape[0]
    ang = pos_ref[...].astype(F32) * inv_ref[...]
    cosv = jnp.cos(ang)
    sinv = jnp.sin(ang) * sgn_ref[...]
    lane = lax.broadcasted_iota(jnp.int32, (tm, LANES), 1)
    first = (lane & (HEAD_DIM - 1)) < (ROPE_DIM // 2)
    low = lane < HEAD_DIM

    def head_rms(t, gmat):
        sq = t * t
        hi = sq.astype(BF16)
        lo = (sq - hi.astype(F32)).astype(BF16)
        msq = (jnp.dot(hi, gmat, preferred_element_type=F32)
               + jnp.dot(lo, gmat, preferred_element_type=F32))
        return t * lax.rsqrt(msq + EPS)

    def rope(t):
        partner = jnp.where(first, pltpu.roll(t, LANES - ROPE_DIM // 2, 1),
                            pltpu.roll(t, ROPE_DIM // 2, 1))
        return t * cosv + partner * sinv

    qn = head_rms(z_scr[:, 0:ATTN_WIDTH], gavg_ref[...]) * qg_ref[...]
    scale = HEAD_DIM ** -0.5
    for cidx in range(ATTN_WIDTH // LANES):
        sl = slice(cidx * LANES, (cidx + 1) * LANES)
        q_ref[:, sl] = (rope(qn[:, sl]) * scale).astype(BF16)

    k0 = ATTN_WIDTH
    kn = head_rms(z_scr[:, k0:k0 + KV_WIDTH], gavg_ref[0:KV_WIDTH, 0:KV_WIDTH]) * kg_ref[...]
    kr = rope(kn)
    v = z_scr[:, k0 + KV_WIDTH:k0 + 2 * KV_WIDTH]
    for base, t in ((0, kr), (4 * LANES, v)):
        a0 = jnp.where(low, t, 0.0)
        b1 = jnp.where(low, 0.0, t)
        b0 = pltpu.roll(a0, HEAD_DIM, 1)
        a1 = pltpu.roll(b1, HEAD_DIM, 1)
        for i, piece in enumerate((a0, b0, a1, b1)):
            kvx_ref[:, base + i * LANES:base + (i + 1) * LANES] = piece.astype(BF16)

    c0 = k0 + 2 * KV_WIDTH
    cp_ref[...] = z_scr[:, c0:c0 + 3 * CONV_WIDTH + POOL_WIDTH]


def _in_proj(x, mod, g1, win, qg, kg, gavg, pos, inv_lane, sgn_lane, seq, tm):
    t, d = x.shape
    in_width = win.shape[1]
    per_batch = seq // tm
    cpw = 3 * CONV_WIDTH + POOL_WIDTH
    const = lambda i: (0, 0)
    return pl.pallas_call(
        _in_kernel,
        grid=(t // tm,),
        in_specs=[pl.BlockSpec((tm, d), lambda i: (i, 0)),
                  pl.BlockSpec((None, 6, d), lambda i: (i // per_batch, 0, 0)),
                  pl.BlockSpec((1, d), const),
                  pl.BlockSpec((d, in_width), const),
                  pl.BlockSpec((1, ATTN_WIDTH), const),
                  pl.BlockSpec((1, KV_WIDTH), const),
                  pl.BlockSpec((ATTN_WIDTH, ATTN_WIDTH), const),
                  pl.BlockSpec((tm, 1), lambda i: (i, 0)),
                  pl.BlockSpec((1, LANES), const),
                  pl.BlockSpec((1, LANES), const)],
        out_specs=[pl.BlockSpec((tm, ATTN_WIDTH), lambda i: (i, 0)),
                   pl.BlockSpec((tm, 8 * LANES), lambda i: (i, 0)),
                   pl.BlockSpec((tm, cpw), lambda i: (i, 0))],
        out_shape=[jax.ShapeDtypeStruct((t, ATTN_WIDTH), BF16),
                   jax.ShapeDtypeStruct((t, 8 * LANES), BF16),
                   jax.ShapeDtypeStruct((t, cpw), F32)],
        scratch_shapes=[pltpu.VMEM((tm, in_width), F32)],
        compiler_params=_params(("arbitrary",)),
        name="in_proj",
    )(x, mod, g1, win, qg, kg, gavg, pos, inv_lane, sgn_lane)


def _mix_kernel(sink_ref, q_ref, kvp_ref, kvc_ref, kvn_ref, cpp_ref, cpc_ref, cpn_ref,
                cw_ref, pw_ref, ps_ref, o_ref, e_scr, l1_scr, l2_scr, l3_scr, *, seq):
    n = pl.program_id(1)
    has_prev = n > 0
    has_next = n < pl.num_programs(1) - 1
    rows2 = 2 * BLOCK
    row = lax.broadcasted_iota(jnp.int32, (rows2, BLOCK), 0) & (BLOCK - 1)
    col = lax.broadcasted_iota(jnp.int32, (rows2, BLOCK), 1)
    mask_p = jnp.logical_and(col >= row, has_prev)
    mask_n = jnp.logical_and(col <= row, has_next)
    upper = lax.broadcasted_iota(jnp.int32, (rows2, 1), 0) < BLOCK

    for g in range(N_KV_HEADS):
        gc = g * 2 * LANES
        q4 = jnp.concatenate([q_ref[:, gc:gc + LANES], q_ref[:, gc + LANES:gc + 2 * LANES]], axis=0)
        acc = None
        for xh in range(2):
            kc = (2 * g + xh) * LANES
            vc = 4 * LANES + kc
            sp = lax.dot_general(q4, kvp_ref[:, kc:kc + LANES], _NT, preferred_element_type=F32)
            sc = lax.dot_general(q4, kvc_ref[:, kc:kc + LANES], _NT, preferred_element_type=F32)
            sn = lax.dot_general(q4, kvn_ref[:, kc:kc + LANES], _NT, preferred_element_type=F32)
            sp = jnp.where(mask_p, sp, MASK_VALUE)
            sn = jnp.where(mask_n, sn, MASK_VALUE)
            sink = jnp.where(upper, sink_ref[4 * g + xh], sink_ref[4 * g + 2 + xh])
            m = jnp.maximum(jnp.maximum(jnp.max(sp, axis=1, keepdims=True),
                                        jnp.max(sc, axis=1, keepdims=True)),
                            jnp.maximum(jnp.max(sn, axis=1, keepdims=True), sink))
            pp = jnp.exp(sp - m)
            pc = jnp.exp(sc - m)
            pn = jnp.exp(sn - m)
            denom = (jnp.sum(pp, axis=1, keepdims=True) + jnp.sum(pc, axis=1, keepdims=True)
                     + jnp.sum(pn, axis=1, keepdims=True) + jnp.exp(sink - m))
            o = (jnp.dot(pp.astype(BF16), kvp_ref[:, vc:vc + LANES], preferred_element_type=F32)
                 + jnp.dot(pc.astype(BF16), kvc_ref[:, vc:vc + LANES], preferred_element_type=F32)
                 + jnp.dot(pn.astype(BF16), kvn_ref[:, vc:vc + LANES], preferred_element_type=F32))
            o = o / denom
            acc = o if acc is None else acc + o
        o_ref[:, gc:gc + LANES] = acc[0:BLOCK].astype(BF16)
        o_ref[:, gc + LANES:gc + 2 * LANES] = acc[BLOCK:rows2].astype(BF16)

    h8 = POOL_HALO
    cw2 = CONV_WIDTH + POOL_WIDTH
    zeros8 = jnp.zeros((h8, cw2), F32)
    for scr in (e_scr, l1_scr, l2_scr, l3_scr):
        scr[0:h8, :] = zeros8
        scr[BLOCK + 3 * h8:BLOCK + 4 * h8, :] = zeros8

    def ext_cols(ref):
        return ref[:, 2 * CONV_WIDTH:3 * CONV_WIDTH] * ref[:, 0:CONV_WIDTH], ref[:, 3 * CONV_WIDTH:]

    up, pxp = ext_cols(cpp_ref)
    un, pxn = ext_cols(cpn_ref)
    uc, pxc = ext_cols(cpc_ref)
    e_scr[h8:2 * h8, 0:CONV_WIDTH] = jnp.where(has_prev, up, 0.0)
    e_scr[h8:2 * h8, CONV_WIDTH:] = jnp.where(has_prev, pxp, 0.0)
    e_scr[2 * h8:2 * h8 + BLOCK, 0:CONV_WIDTH] = uc
    e_scr[2 * h8:2 * h8 + BLOCK, CONV_WIDTH:] = pxc
    e_scr[2 * h8 + BLOCK:3 * h8 + BLOCK, 0:CONV_WIDTH] = jnp.where(has_next, un, 0.0)
    e_scr[2 * h8 + BLOCK:3 * h8 + BLOCK, CONV_WIDTH:] = jnp.where(has_next, pxn, 0.0)

    t0 = 2 * h8
    u_m1 = e_scr[t0 - 1:t0 - 1 + BLOCK, 0:CONV_WIDTH]
    u_p1 = e_scr[t0 + 1:t0 + 1 + BLOCK, 0:CONV_WIDTH]
    conv = cpc_ref[:, CONV_WIDTH:2 * CONV_WIDTH] * (
        cw_ref[0:1, :] * u_m1 + cw_ref[1:2, :] * uc + cw_ref[2:3, :] * u_p1)
    o_ref[:, ATTN_WIDTH:ATTN_WIDTH + CONV_WIDTH] = conv.astype(BF16)

    ext = BLOCK + 2 * h8
    pc0 = CONV_WIDTH
    l1_scr[h8:h8 + ext, pc0:] = e_scr[h8 - 1:h8 - 1 + ext, pc0:] + e_scr[h8:h8 + ext, pc0:]
    l2_scr[h8:h8 + ext, pc0:] = l1_scr[h8 - 1:h8 - 1 + ext, pc0:] + l1_scr[h8 + 1:h8 + 1 + ext, pc0:]
    l3_scr[h8:h8 + ext, pc0:] = l2_scr[h8 - 2:h8 - 2 + ext, pc0:] + l2_scr[h8 + 2:h8 + 2 + ext, pc0:]
    w2 = l1_scr[t0:t0 + BLOCK, pc0:]
    w4 = l2_scr[t0:t0 + BLOCK, pc0:]
    w8 = l3_scr[t0:t0 + BLOCK, pc0:]
    w16 = l3_scr[t0 - 4:t0 - 4 + BLOCK, pc0:] + l3_scr[t0 + 4:t0 + 4 + BLOCK, pc0:]
    lane = lax.broadcasted_iota(jnp.int32, (BLOCK, POOL_WIDTH), 1)
    wsum = jnp.where(lane < 64, w2, jnp.where(lane < 128, w4, jnp.where(lane < 192, w8, w16)))
    half = jnp.where(lane < 64, 1, jnp.where(lane < 128, 2, jnp.where(lane < 192, 4, 8)))
    tpos = n * BLOCK + lax.broadcasted_iota(jnp.int32, (BLOCK, POOL_WIDTH), 0)
    cnt = jnp.minimum(tpos + half, seq) - jnp.maximum(tpos - half, 0)
    pooled = wsum / cnt.astype(F32) - pxc
    pool = jnp.dot(pooled.astype(BF16), pw_ref[...], preferred_element_type=F32) * ps_ref[...]
    o_ref[:, ATTN_WIDTH + CONV_WIDTH:] = pool.astype(BF16)


def _mix(sink, q, kvx, cp, conv_w, pool_bd, pool_scale, batch, seq):
    t = q.shape[0]
    nblk = seq // BLOCK
    sub = BLOCK // POOL_HALO
    cpw = cp.shape[1]
    d_mix = ATTN_WIDTH + CONV_WIDTH + POOL_WIDTH
    cw2 = CONV_WIDTH + POOL_WIDTH
    rows = BLOCK + 4 * POOL_HALO

    def blk(b, n):
        return b * nblk + n

    return pl.pallas_call(
        functools.partial(_mix_kernel, seq=seq),
        grid=(batch, nblk),
        in_specs=[
            pl.BlockSpec(memory_space=pltpu.SMEM),
            pl.BlockSpec((BLOCK, ATTN_WIDTH), lambda b, n: (blk(b, n), 0)),
            pl.BlockSpec((BLOCK, 8 * LANES), lambda b, n: (blk(b, jnp.maximum(n - 1, 0)), 0)),
            pl.BlockSpec((BLOCK, 8 * LANES), lambda b, n: (blk(b, n), 0)),
            pl.BlockSpec((BLOCK, 8 * LANES), lambda b, n: (blk(b, jnp.minimum(n + 1, nblk - 1)), 0)),
            pl.BlockSpec((POOL_HALO, cpw), lambda b, n: (jnp.maximum(blk(b, n) * sub - 1, 0), 0)),
            pl.BlockSpec((BLOCK, cpw), lambda b, n: (blk(b, n), 0)),
            pl.BlockSpec((POOL_HALO, cpw),
                         lambda b, n: (jnp.minimum((blk(b, n) + 1) * sub, t // POOL_HALO - 1), 0)),
            pl.BlockSpec((3, CONV_WIDTH), lambda b, n: (0, 0)),
            pl.BlockSpec((POOL_WIDTH, POOL_WIDTH), lambda b, n: (0, 0)),
            pl.BlockSpec((1, POOL_WIDTH), lambda b, n: (0, 0)),
        ],
        out_specs=pl.BlockSpec((BLOCK, d_mix), lambda b, n: (blk(b, n), 0)),
        out_shape=jax.ShapeDtypeStruct((t, d_mix), BF16),
        scratch_shapes=[pltpu.VMEM((rows, cw2), F32)] * 4,
        compiler_params=_params(("arbitrary", "arbitrary")),
        name="mix",
    )(sink, q, kvx, kvx, kvx, cp, cp, cp, conv_w, pool_bd, pool_scale)


def _out_kernel(x_ref, mix_ref, mod_ref, wout_ref, g2_ref, wq_ref, x1_ref, h2_ref, q_ref):
    proj = jnp.dot(mix_ref[...], wout_ref[...], preferred_element_type=F32)
    x1 = x_ref[...] + mod_ref[2:3, :] * proj
    x1_ref[...] = x1
    ms = jnp.mean(x1 * x1, axis=-1, keepdims=True)
    h2 = x1 * lax.rsqrt(ms + EPS) * g2_ref[...] * (1.0 + mod_ref[4:5, :]) + mod_ref[3:4, :]
    h2b = h2.astype(BF16)
    h2_ref[...] = h2b
    q_ref[...] = jnp.dot(h2b, wq_ref[...], preferred_element_type=F32).astype(BF16)


def _out_proj(x, mix, mod, wout, g2, wq, seq, tm):
    t, d = x.shape
    qw = wq.shape[1]
    per_batch = seq // tm
    const = lambda i: (0, 0)
    return pl.pallas_call(
        _out_kernel,
        grid=(t // tm,),
        in_specs=[pl.BlockSpec((tm, d), lambda i: (i, 0)),
                  pl.BlockSpec((tm, mix.shape[1]), lambda i: (i, 0)),
                  pl.BlockSpec((None, 6, d), lambda i: (i // per_batch, 0, 0)),
                  pl.BlockSpec(wout.shape, const),
                  pl.BlockSpec((1, d), const),
                  pl.BlockSpec((d, qw), const)],
        out_specs=[pl.BlockSpec((tm, d), lambda i: (i, 0)),
                   pl.BlockSpec((tm, d), lambda i: (i, 0)),
                   pl.BlockSpec((tm, qw), lambda i: (i, 0))],
        out_shape=[jax.ShapeDtypeStruct((t, d), F32),
                   jax.ShapeDtypeStruct((t, d), BF16),
                   jax.ShapeDtypeStruct((t, qw), BF16)],
        compiler_params=_params(("arbitrary",)),
        name="out_proj",
    )(x, mix, mod, wout, g2, wq)


def _topk_kernel(q_ref, keys_ref, s1_ref, e1_ref, s2_ref, e2_ref, tau_ref, v_scr, cand_scr):
    half_d = PEER_QDIM // 2
    neg_inf = -jnp.inf
    for h in range(PEER_HEADS):
        rows = slice(h * PEER_NKEYS, (h + 1) * PEER_NKEYS)
        for side, dst in ((0, s1_ref), (1, s2_ref)):
            c0 = h * PEER_QDIM + side * half_d
            s = lax.dot_general(keys_ref[side], q_ref[:, c0:c0 + half_d], _NT,
                                preferred_element_type=F32)
            dst[rows, :] = s
            cur = s
            for k in range(PEER_TOPK):
                m = jnp.max(cur, axis=0, keepdims=True)
                v_scr[side, k:k + 1, :] = m
                cur = jnp.where(cur == m, neg_inf, cur)
        v1 = v_scr[0]
        v2 = v_scr[1]
        cand_scr[0:16, :] = v1[0:1, :] + v2
        for r1 in range(1, 8):
            cand_scr[8 + 8 * r1:16 + 8 * r1, :] = v1[r1:r1 + 1, :] + v2[0:8, :]
        cand_scr[72:80, :] = v1[8:16, :] + v2[0:1, :]
        cand = cand_scr[...]
        cur = cand
        tau = None
        for k in range(PEER_TOPK):
            tau = jnp.max(cur, axis=0, keepdims=True)
            cur = jnp.where(cur == tau, neg_inf, cur)
        top = v1[0:1, :] + v2[0:1, :]
        z = jnp.sum(jnp.where(cand >= tau, jnp.exp(cand - top), 0.0), axis=0, keepdims=True)
        tau_ref[h:h + 1, :] = tau
        e1_ref[rows, :] = jnp.exp(s1_ref[rows, :] - v1[0:1, :]) / z
        e2_ref[rows, :] = jnp.exp(s2_ref[rows, :] - v2[0:1, :])


def _topk(q, keys, tt):
    t, qw = q.shape
    nrow = PEER_HEADS * PEER_NKEYS
    big = pl.BlockSpec((nrow, tt), lambda i: (0, i))
    big_shape = jax.ShapeDtypeStruct((nrow, t), F32)
    return pl.pallas_call(
        _topk_kernel,
        grid=(t // tt,),
        in_specs=[pl.BlockSpec((tt, qw), lambda i: (i, 0)),
                  pl.BlockSpec(keys.shape, lambda i: (0, 0, 0))],
        out_specs=[big, big, big, big, pl.BlockSpec((PEER_HEADS, tt), lambda i: (0, i))],
        out_shape=[big_shape, big_shape, big_shape, big_shape,
                   jax.ShapeDtypeStruct((PEER_HEADS, t), F32)],
        scratch_shapes=[pltpu.VMEM((2, PEER_TOPK, tt), F32), pltpu.VMEM((80, tt), F32)],
        compiler_params=_params(("arbitrary",)),
        name="peer_topk",
    )(q, keys)


DENSE_ROWS = 16
DENSE_LANES = 256


def _dense_kernel(h2_ref, u_ref, vt_ref, s1_ref, e1_ref, s2_ref, e2_ref, tau_ref, x1_ref,
                  mod_ref, o_ref, yt_scr, p_scr, wa_scr):
    j = pl.program_id(1)
    ec, tm = p_scr.shape
    n_a = ec // PEER_NKEYS
    n_c = tm // DENSE_LANES

    @pl.when(j == 0)
    def _():
        yt_scr[...] = jnp.zeros_like(yt_scr)

    p_scr[...] = lax.dot_general(u_ref[...], h2_ref[...], _NT, preferred_element_type=F32)

    def tile(it, carry):
        r0 = pl.multiple_of((it // n_c) * DENSE_ROWS, DENSE_ROWS)
        c0 = pl.multiple_of((it % n_c) * DENSE_LANES, DENSE_LANES)
        cols = pl.ds(c0, DENSE_LANES)
        accs = [jnp.zeros((DENSE_ROWS, DENSE_LANES), F32) for _ in range(n_a)]
        for h in range(PEER_HEADS):
            jrows = pl.ds(h * PEER_NKEYS + r0, DENSE_ROWS)
            s2 = s2_ref[jrows, cols]
            e2 = e2_ref[jrows, cols]
            tau = tau_ref[h:h + 1, cols]
            for al in range(n_a):
                arow = pl.ds(h * PEER_NKEYS + j * n_a + al, 1)
                sel = (s2 + s1_ref[arow, cols]) >= tau
                accs[al] = accs[al] + jnp.where(sel, e2, 0.0) * e1_ref[arow, cols]
        for al in range(n_a):
            erows = pl.ds(al * PEER_NKEYS + r0, DENSE_ROWS)
            p = p_scr[erows, cols]
            act = 0.5 * p * (1.0 + lax.erf(p * INV_SQRT2))
            wa_scr[erows, cols] = (accs[al] * act).astype(BF16)
        return carry

    lax.fori_loop(0, (PEER_NKEYS // DENSE_ROWS) * n_c, tile, 0)

    yt_scr[...] += jnp.dot(vt_ref[...], wa_scr[...], preferred_element_type=F32)

    @pl.when(j == pl.num_programs(1) - 1)
    def _():
        o_ref[...] = x1_ref[...] + mod_ref[5:6, :] * yt_scr[...].T


def _dense(h2, u, vt, s1, e1, s2, e2, tau, x1, mod, seq, tm, ec):
    t, d = x1.shape
    n_exp = u.shape[0]
    nrow = PEER_HEADS * PEER_NKEYS
    per_batch = seq // tm
    tok = pl.BlockSpec((tm, d), lambda i, j: (i, 0))
    big = pl.BlockSpec((nrow, tm), lambda i, j: (0, i))
    return pl.pallas_call(
        _dense_kernel,
        grid=(t // tm, n_exp // ec),
        in_specs=[tok,
                  pl.BlockSpec((ec, d), lambda i, j: (j, 0)),
                  pl.BlockSpec((d, ec), lambda i, j: (0, j)),
                  big, big, big, big,
                  pl.BlockSpec((PEER_HEADS, tm), lambda i, j: (0, i)),
                  tok,
                  pl.BlockSpec((None, 6, d), lambda i, j: (i // per_batch, 0, 0))],
        out_specs=tok,
        out_shape=jax.ShapeDtypeStruct((t, d), F32),
        scratch_shapes=[pltpu.VMEM((d, tm), F32), pltpu.VMEM((ec, tm), F32),
                        pltpu.VMEM((ec, tm), BF16)],
        compiler_params=_params(("arbitrary", "arbitrary")),
        name="peer_dense",
    )(h2, u, vt, s1, e1, s2, e2, tau, x1, mod)


def _tile(n, pref):
    return pref if n % pref == 0 else n


def kernel(x, c, positions, norm1_g, norm2_g, w_ada, b_ada, w_in, q_norm_g, k_norm_g, attn_sink,
           conv_w, pool_w, pool_scale, w_out, peer_wq, peer_keys, peer_u, peer_v):
    batch, seq, d = x.shape
    t = batch * seq
    n_layers = w_ada.shape[0]
    tm = _tile(seq, 512)
    tt = _tile(seq, 256)
    ec = 512

    inv = ROPE_THETA ** (-jnp.arange(0, ROPE_DIM, 2, dtype=F32) / ROPE_DIM)
    lane = jnp.arange(LANES) % HEAD_DIM
    inv_lane = jnp.where(lane < ROPE_DIM, inv[lane % (ROPE_DIM // 2)], 0.0).reshape(1, LANES)
    sgn_lane = jnp.where(lane < ROPE_DIM // 2, -1.0, 1.0).astype(F32).reshape(1, LANES)
    head_id = jnp.arange(ATTN_WIDTH) // HEAD_DIM
    gavg = jnp.where(head_id[:, None] == head_id[None, :], 1.0 / HEAD_DIM, 0.0).astype(BF16)
    pos = positions.reshape(t, 1)

    mod_all = _ada(c, w_ada, b_ada).reshape(n_layers, batch, 6, d)
    xt = x.reshape(t, d)
    for l in range(n_layers):
        mod = mod_all[l]
        qg = jnp.tile(q_norm_g[l], N_Q_HEADS).reshape(1, ATTN_WIDTH)
        kg = jnp.tile(k_norm_g[l], N_KV_HEADS).reshape(1, KV_WIDTH)
        q, kvx, cp = _in_proj(xt, mod, norm1_g[l].reshape(1, d), w_in[l].astype(BF16), qg, kg, gavg,
                              pos, inv_lane, sgn_lane, seq, tm)
        pool_bd = jax.scipy.linalg.block_diag(*[pool_w[l, g] for g in range(pool_w.shape[1])])
        mix = _mix(attn_sink[l], q, kvx, cp, conv_w[l], pool_bd.astype(BF16),
                   pool_scale[l].reshape(1, POOL_WIDTH), batch, seq)
        x1, h2, pq = _out_proj(xt, mix, mod, w_out[l].astype(BF16), norm2_g[l].reshape(1, d),
                               peer_wq[l].astype(BF16), seq, tm)
        s1, e1, s2, e2, tau = _topk(pq, peer_keys[l].astype(BF16), tt)
        xt = _dense(h2, peer_u[l].astype(BF16), peer_v[l].T.astype(BF16), s1, e1, s2, e2, tau,
                    x1, mod, seq, tm, ec)
    return xt.reshape(batch, seq, d)
```

```python
import functools
import math

import jax
import jax.numpy as jnp
from jax import lax
from jax.experimental import pallas as pl
from jax.experimental.pallas import tpu as pltpu

F32 = jnp.float32
BF16 = jnp.bfloat16

HEAD_DIM = 64
N_Q_HEADS = 8
N_KV_HEADS = 2
ATTN_WIDTH = N_Q_HEADS * HEAD_DIM
KV_WIDTH = N_KV_HEADS * HEAD_DIM
BLOCK = 128
ROPE_THETA = 500000.0
ROPE_DIM = 16
CONV_WIDTH = 256
POOL_WIDTH = 256
POOL_GROUP = 64
POOL_HALO = 8
PEER_HEADS = 8
PEER_NKEYS = 128
PEER_QDIM = 256
PEER_TOPK = 16
EPS = 1e-6
MASK_VALUE = -1e30
INV_SQRT2 = 1.0 / math.sqrt(2.0)

LANES = 128
VMEM_LIMIT = 48 * 1024 * 1024

_NT = (((1,), (1,)), ((), ()))


def _params(sem):
    return pltpu.CompilerParams(dimension_semantics=sem, vmem_limit_bytes=VMEM_LIMIT)


def _ada_kernel(c_ref, w_ref, b_ref, o_ref):
    c = c_ref[...]
    c_act = c * (1.0 / (1.0 + jnp.exp(-c)))
    o_ref[...] = jnp.dot(c_act.astype(BF16), w_ref[...].astype(BF16),
                         preferred_element_type=F32) + b_ref[...]


def _ada(c, w_ada, b_ada):
    n_layers, d, d6 = w_ada.shape
    b = c.shape[0]
    return pl.pallas_call(
        _ada_kernel,
        grid=(n_layers, d6 // d),
        in_specs=[pl.BlockSpec((b, d), lambda l, j: (0, 0)),
                  pl.BlockSpec((None, d, d), lambda l, j: (l, 0, j)),
                  pl.BlockSpec((None, 1, d), lambda l, j: (l, 0, j))],
        out_specs=pl.BlockSpec((None, b, d), lambda l, j: (l, 0, j)),
        out_shape=jax.ShapeDtypeStruct((n_layers, b, d6), F32),
        compiler_params=_params(("arbitrary", "arbitrary")),
        name="ada",
    )(c, w_ada, b_ada.reshape(n_layers, 1, d6))


def _in_kernel(x_ref, mod_ref, g1_ref, win_ref, qg_ref, kg_ref, gavg_ref, pos_ref,
               inv_ref, sgn_ref, q_ref, kvx_ref, cp_ref, z_scr):
    x = x_ref[...]
    ms = jnp.mean(x * x, axis=-1, keepdims=True)
    y = x * lax.rsqrt(ms + EPS) * g1_ref[...]
    h = y * (1.0 + mod_ref[1:2, :]) + mod_ref[0:1, :]
    z_scr[...] = jnp.dot(h.astype(BF16), win_ref[...], preferred_element_type=F32)

    tm = x.shape[0]
    ang = pos_ref[...].astype(F32) * inv_ref[...]
    cosv = jnp.cos(ang)
    sinv = jnp.sin(ang) * sgn_ref[...]
    lane = lax.broadcasted_iota(jnp.int32, (tm, LANES), 1)
    first = (lane & (HEAD_DIM - 1)) < (ROPE_DIM // 2)
    low = lane < HEAD_DIM

    def head_rms(t, gmat):
        sq = t * t
        hi = sq.astype(BF16)
        lo = (sq - hi.astype(F32)).astype(BF16)
        msq = (jnp.dot(hi, gmat, preferred_element_type=F32)
               + jnp.dot(lo, gmat, preferred_element_type=F32))
        return t * lax.rsqrt(msq + EPS)

    def rope(t):
        partner = jnp.where(first, pltpu.roll(t, LANES - ROPE_DIM // 2, 1),
                            pltpu.roll(t, ROPE_DIM // 2, 1))
        return t * cosv + partner * sinv

    qn = head_rms(z_scr[:, 0:ATTN_WIDTH], gavg_ref[...]) * qg_ref[...]
    scale = HEAD_DIM ** -0.5
    for cidx in range(ATTN_WIDTH // LANES):
        sl = slice(cidx * LANES, (cidx + 1) * LANES)
        q_ref[:, sl] = (rope(qn[:, sl]) * scale).astype(BF16)

    k0 = ATTN_WIDTH
    kn = head_rms(z_scr[:, k0:k0 + KV_WIDTH], gavg_ref[0:KV_WIDTH, 0:KV_WIDTH]) * kg_ref[...]
    kr = rope(kn)
    v = z_scr[:, k0 + KV_WIDTH:k0 + 2 * KV_WIDTH]
    for base, t in ((0, kr), (4 * LANES, v)):
        a0 = jnp.where(low, t, 0.0)
        b1 = jnp.where(low, 0.0, t)
        b0 = pltpu.roll(a0, HEAD_DIM, 1)
        a1 = pltpu.roll(b1, HEAD_DIM, 1)
        for i, piece in enumerate((a0, b0, a1, b1)):
            kvx_ref[:, base + i * LANES:base + (i + 1) * LANES] = piece.astype(BF16)

    c0 = k0 + 2 * KV_WIDTH
    cp_ref[...] = z_scr[:, c0:c0 + 3 * CONV_WIDTH + POOL_WIDTH]


def _in_proj(x, mod, g1, win, qg, kg, gavg, pos, inv_lane, sgn_lane, seq, tm):
    t, d = x.shape
    in_width = win.shape[1]
    per_batch = seq // tm
    cpw = 3 * CONV_WIDTH + POOL_WIDTH
    const = lambda i: (0, 0)
    return pl.pallas_call(
        _in_kernel,
        grid=(t // tm,),
        in_specs=[pl.BlockSpec((tm, d), lambda i: (i, 0)),
                  pl.BlockSpec((None, 6, d), lambda i: (i // per_batch, 0, 0)),
                  pl.BlockSpec((1, d), const),
                  pl.BlockSpec((d, in_width), const),
                  pl.BlockSpec((1, ATTN_WIDTH), const),
                  pl.BlockSpec((1, KV_WIDTH), const),
                  pl.BlockSpec((ATTN_WIDTH, ATTN_WIDTH), const),
                  pl.BlockSpec((tm, 1), lambda i: (i, 0)),
                  pl.BlockSpec((1, LANES), const),
                  pl.BlockSpec((1, LANES), const)],
        out_specs=[pl.BlockSpec((tm, ATTN_WIDTH), lambda i: (i, 0)),
                   pl.BlockSpec((tm, 8 * LANES), lambda i: (i, 0)),
                   pl.BlockSpec((tm, cpw), lambda i: (i, 0))],
        out_shape=[jax.ShapeDtypeStruct((t, ATTN_WIDTH), BF16),
                   jax.ShapeDtypeStruct((t, 8 * LANES), BF16),
                   jax.ShapeDtypeStruct((t, cpw), F32)],
        scratch_shapes=[pltpu.VMEM((tm, in_width), F32)],
        compiler_params=_params(("arbitrary",)),
        name="in_proj",
    )(x, mod, g1, win, qg, kg, gavg, pos, inv_lane, sgn_lane)


def _mix_kernel(sink_ref, q_ref, kvp_ref, kvc_ref, kvn_ref, cpp_ref, cpc_ref, cpn_ref,
                cw_ref, pw_ref, ps_ref, o_ref, e_scr, l1_scr, l2_scr, l3_scr, *, seq):
    n = pl.program_id(1)
    has_prev = n > 0
    has_next = n < pl.num_programs(1) - 1
    rows2 = 2 * BLOCK
    row = lax.broadcasted_iota(jnp.int32, (rows2, BLOCK), 0) & (BLOCK - 1)
    col = lax.broadcasted_iota(jnp.int32, (rows2, BLOCK), 1)
    mask_p = jnp.logical_and(col >= row, has_prev)
    mask_n = jnp.logical_and(col <= row, has_next)
    upper = lax.broadcasted_iota(jnp.int32, (rows2, 1), 0) < BLOCK

    for g in range(N_KV_HEADS):
        gc = g * 2 * LANES
        q4 = jnp.concatenate([q_ref[:, gc:gc + LANES], q_ref[:, gc + LANES:gc + 2 * LANES]], axis=0)
        acc = None
        for xh in range(2):
            kc = (2 * g + xh) * LANES
            vc = 4 * LANES + kc
            sp = lax.dot_general(q4, kvp_ref[:, kc:kc + LANES], _NT, preferred_element_type=F32)
            sc = lax.dot_general(q4, kvc_ref[:, kc:kc + LANES], _NT, preferred_element_type=F32)
            sn = lax.dot_general(q4, kvn_ref[:, kc:kc + LANES], _NT, preferred_element_type=F32)
            sp = jnp.where(mask_p, sp, MASK_VALUE)
            sn = jnp.where(mask_n, sn, MASK_VALUE)
            sink = jnp.where(upper, sink_ref[4 * g + xh], sink_ref[4 * g + 2 + xh])
            m = jnp.maximum(jnp.maximum(jnp.max(sp, axis=1, keepdims=True),
                                        jnp.max(sc, axis=1, keepdims=True)),
                            jnp.maximum(jnp.max(sn, axis=1, keepdims=True), sink))
            pp = jnp.exp(sp - m)
            pc = jnp.exp(sc - m)
            pn = jnp.exp(sn - m)
            denom = (jnp.sum(pp, axis=1, keepdims=True) + jnp.sum(pc, axis=1, keepdims=True)
                     + jnp.sum(pn, axis=1, keepdims=True) + jnp.exp(sink - m))
            o = (jnp.dot(pp.astype(BF16), kvp_ref[:, vc:vc + LANES], preferred_element_type=F32)
                 + jnp.dot(pc.astype(BF16), kvc_ref[:, vc:vc + LANES], preferred_element_type=F32)
                 + jnp.dot(pn.astype(BF16), kvn_ref[:, vc:vc + LANES], preferred_element_type=F32))
            o = o / denom
            acc = o if acc is None else acc + o
        o_ref[:, gc:gc + LANES] = acc[0:BLOCK].astype(BF16)
        o_ref[:, gc + LANES:gc + 2 * LANES] = acc[BLOCK:rows2].astype(BF16)

    h8 = POOL_HALO
    cw2 = CONV_WIDTH + POOL_WIDTH
    zeros8 = jnp.zeros((h8, cw2), F32)
    for scr in (e_scr, l1_scr, l2_scr, l3_scr):
        scr[0:h8, :] = zeros8
        scr[BLOCK + 3 * h8:BLOCK + 4 * h8, :] = zeros8

    def ext_cols(ref):
        return ref[:, 2 * CONV_WIDTH:3 * CONV_WIDTH] * ref[:, 0:CONV_WIDTH], ref[:, 3 * CONV_WIDTH:]

    up, pxp = ext_cols(cpp_ref)
    un, pxn = ext_cols(cpn_ref)
    uc, pxc = ext_cols(cpc_ref)
    e_scr[h8:2 * h8, 0:CONV_WIDTH] = jnp.where(has_prev, up, 0.0)
    e_scr[h8:2 * h8, CONV_WIDTH:] = jnp.where(has_prev, pxp, 0.0)
    e_scr[2 * h8:2 * h8 + BLOCK, 0:CONV_WIDTH] = uc
    e_scr[2 * h8:2 * h8 + BLOCK, CONV_WIDTH:] = pxc
    e_scr[2 * h8 + BLOCK:3 * h8 + BLOCK, 0:CONV_WIDTH] = jnp.where(has_next, un, 0.0)
    e_scr[2 * h8 + BLOCK:3 * h8 + BLOCK, CONV_WIDTH:] = jnp.where(has_next, pxn, 0.0)

    t0 = 2 * h8
    u_m1 = e_scr[t0 - 1:t0 - 1 + BLOCK, 0:CONV_WIDTH]
    u_p1 = e_scr[t0 + 1:t0 + 1 + BLOCK, 0:CONV_WIDTH]
    conv = cpc_ref[:, CONV_WIDTH:2 * CONV_WIDTH] * (
        cw_ref[0:1, :] * u_m1 + cw_ref[1:2, :] * uc + cw_ref[2:3, :] * u_p1)
    o_ref[:, ATTN_WIDTH:ATTN_WIDTH + CONV_WIDTH] = conv.astype(BF16)

    ext = BLOCK + 2 * h8
    pc0 = CONV_WIDTH
    l1_scr[h8:h8 + ext, pc0:] = e_scr[h8 - 1:h8 - 1 + ext, pc0:] + e_scr[h8:h8 + ext, pc0:]
    l2_scr[h8:h8 + ext, pc0:] = l1_scr[h8 - 1:h8 - 1 + ext, pc0:] + l1_scr[h8 + 1:h8 + 1 + ext, pc0:]
    l3_scr[h8:h8 + ext, pc0:] = l2_scr[h8 - 2:h8 - 2 + ext, pc0:] + l2_scr[h8 + 2:h8 + 2 + ext, pc0:]
    w2 = l1_scr[t0:t0 + BLOCK, pc0:]
    w4 = l2_scr[t0:t0 + BLOCK, pc0:]
    w8 = l3_scr[t0:t0 + BLOCK, pc0:]
    w16 = l3_scr[t0 - 4:t0 - 4 + BLOCK, pc0:] + l3_scr[t0 + 4:t0 + 4 + BLOCK, pc0:]
    lane = lax.broadcasted_iota(jnp.int32, (BLOCK, POOL_WIDTH), 1)
    wsum = jnp.where(lane < 64, w2, jnp.where(lane < 128, w4, jnp.where(lane < 192, w8, w16)))
    half = jnp.where(lane < 64, 1, jnp.where(lane < 128, 2, jnp.where(lane < 192, 4, 8)))
    tpos = n * BLOCK + lax.broadcasted_iota(jnp.int32, (BLOCK, POOL_WIDTH), 0)
    cnt = jnp.minimum(tpos + half, seq) - jnp.maximum(tpos - half, 0)
    pooled = wsum / cnt.astype(F32) - pxc
    pool = jnp.dot(pooled.astype(BF16), pw_ref[...], preferred_element_type=F32) * ps_ref[...]
    o_ref[:, ATTN_WIDTH + CONV_WIDTH:] = pool.astype(BF16)


def _mix(sink, q, kvx, cp, conv_w, pool_bd, pool_scale, batch, seq):
    t = q.shape[0]
    nblk = seq // BLOCK
    sub = BLOCK // POOL_HALO
    cpw = cp.shape[1]
    d_mix = ATTN_WIDTH + CONV_WIDTH + POOL_WIDTH
    cw2 = CONV_WIDTH + POOL_WIDTH
    rows = BLOCK + 4 * POOL_HALO

    def blk(b, n):
        return b * nblk + n

    return pl.pallas_call(
        functools.partial(_mix_kernel, seq=seq),
        grid=(batch, nblk),
        in_specs=[
            pl.BlockSpec(memory_space=pltpu.SMEM),
            pl.BlockSpec((BLOCK, ATTN_WIDTH), lambda b, n: (blk(b, n), 0)),
            pl.BlockSpec((BLOCK, 8 * LANES), lambda b, n: (blk(b, jnp.maximum(n - 1, 0)), 0)),
            pl.BlockSpec((BLOCK, 8 * LANES), lambda b, n: (blk(b, n), 0)),
            pl.BlockSpec((BLOCK, 8 * LANES), lambda b, n: (blk(b, jnp.minimum(n + 1, nblk - 1)), 0)),
            pl.BlockSpec((POOL_HALO, cpw), lambda b, n: (jnp.maximum(blk(b, n) * sub - 1, 0), 0)),
            pl.BlockSpec((BLOCK, cpw), lambda b, n: (blk(b, n), 0)),
            pl.BlockSpec((POOL_HALO, cpw),
                         lambda b, n: (jnp.minimum((blk(b, n) + 1) * sub, t // POOL_HALO - 1), 0)),
            pl.BlockSpec((3, CONV_WIDTH), lambda b, n: (0, 0)),
            pl.BlockSpec((POOL_WIDTH, POOL_WIDTH), lambda b, n: (0, 0)),
            pl.BlockSpec((1, POOL_WIDTH), lambda b, n: (0, 0)),
        ],
        out_specs=pl.BlockSpec((BLOCK, d_mix), lambda b, n: (blk(b, n), 0)),
        out_shape=jax.ShapeDtypeStruct((t, d_mix), BF16),
        scratch_shapes=[pltpu.VMEM((rows, cw2), F32)] * 4,
        compiler_params=_params(("arbitrary", "arbitrary")),
        name="mix",
    )(sink, q, kvx, kvx, kvx, cp, cp, cp, conv_w, pool_bd, pool_scale)


def _out_kernel(x_ref, mix_ref, mod_ref, wout_ref, g2_ref, wq_ref, x1_ref, h2_ref, q_ref):
    proj = jnp.dot(mix_ref[...], wout_ref[...], preferred_element_type=F32)
    x1 = x_ref[...] + mod_ref[2:3, :] * proj
    x1_ref[...] = x1
    ms = jnp.mean(x1 * x1, axis=-1, keepdims=True)
    h2 = x1 * lax.rsqrt(ms + EPS) * g2_ref[...] * (1.0 + mod_ref[4:5, :]) + mod_ref[3:4, :]
    h2b = h2.astype(BF16)
    h2_ref[...] = h2b
    q_ref[...] = jnp.dot(h2b, wq_ref[...], preferred_element_type=F32).astype(BF16)


def _out_proj(x, mix, mod, wout, g2, wq, seq, tm):
    t, d = x.shape
    qw = wq.shape[1]
    per_batch = seq // tm
    const = lambda i: (0, 0)
    return pl.pallas_call(
        _out_kernel,
        grid=(t // tm,),
        in_specs=[pl.BlockSpec((tm, d), lambda i: (i, 0)),
                  pl.BlockSpec((tm, mix.shape[1]), lambda i: (i, 0)),
                  pl.BlockSpec((None, 6, d), lambda i: (i // per_batch, 0, 0)),
                  pl.BlockSpec(wout.shape, const),
                  pl.BlockSpec((1, d), const),
                  pl.BlockSpec((d, qw), const)],
        out_specs=[pl.BlockSpec((tm, d), lambda i: (i, 0)),
                   pl.BlockSpec((tm, d), lambda i: (i, 0)),
                   pl.BlockSpec((tm, qw), lambda i: (i, 0))],
        out_shape=[jax.ShapeDtypeStruct((t, d), F32),
                   jax.ShapeDtypeStruct((t, d), BF16),
                   jax.ShapeDtypeStruct((t, qw), BF16)],
        compiler_params=_params(("arbitrary",)),
        name="out_proj",
    )(x, mix, mod, wout, g2, wq)


def _topk_kernel(q_ref, keys_ref, r2_ref, e2_ref, cnt_ref, e1_ref, v_scr, cand_scr, c_scr):
    half_d = PEER_QDIM // 2
    neg_inf = -jnp.inf
    for h in range(PEER_HEADS):
        rows = slice(h * PEER_NKEYS, (h + 1) * PEER_NKEYS)
        for side in range(2):
            c0 = h * PEER_QDIM + side * half_d
            s = lax.dot_general(keys_ref[side], q_ref[:, c0:c0 + half_d], _NT,
                                preferred_element_type=F32)
            cur = s
            rank = jnp.full(s.shape, float(PEER_TOPK), F32)
            for k in range(PEER_TOPK):
                m = jnp.max(cur, axis=0, keepdims=True)
                if k == 0:
                    gate = jnp.exp(s - m)
                    if side == 0:
                        e1_ref[rows, :] = gate
                    else:
                        e2_ref[rows, :] = gate.astype(BF16)
                v_scr[side, k:k + 1, :] = m
                hit = cur == m
                rank = jnp.where(hit, float(k), rank)
                cur = jnp.where(hit, neg_inf, cur)
            if side == 0:
                cnt_ref[rows, :] = rank
            else:
                r2_ref[rows, :] = rank.astype(BF16)
        v1 = v_scr[0]
        v2 = v_scr[1]
        cand_scr[0:16, :] = v1[0:1, :] + v2
        for r1 in range(1, 8):
            cand_scr[8 + 8 * r1:16 + 8 * r1, :] = v1[r1:r1 + 1, :] + v2[0:8, :]
        cand_scr[72:80, :] = v1[8:16, :] + v2[0:1, :]
        cand = cand_scr[...]
        cur = cand
        for k in range(PEER_TOPK):
            m = jnp.max(cur, axis=0, keepdims=True)
            cur = jnp.where(cur == m, neg_inf, cur)
        sel = cur == neg_inf
        top = v1[0:1, :] + v2[0:1, :]
        z = jnp.sum(jnp.where(sel, jnp.exp(cand - top), 0.0), axis=0, keepdims=True)
        self = jnp.where(sel, 1.0, 0.0)
        c_scr[0:1, :] = jnp.sum(self[0:16], axis=0, keepdims=True)
        for r1 in range(1, 8):
            c_scr[r1:r1 + 1, :] = jnp.sum(self[8 + 8 * r1:16 + 8 * r1], axis=0, keepdims=True)
        c_scr[8:16, :] = self[72:80]
        rank1 = cnt_ref[rows, :]
        cnt = jnp.zeros_like(rank1)
        for r in range(PEER_TOPK):
            cnt = jnp.where(rank1 == float(r), c_scr[r:r + 1, :], cnt)
        cnt_ref[rows, :] = cnt
        e1_ref[rows, :] = e1_ref[rows, :] / z


def _topk(q, keys, tt):
    t, qw = q.shape
    nrow = PEER_HEADS * PEER_NKEYS
    big = pl.BlockSpec((nrow, tt), lambda i: (0, i))
    return pl.pallas_call(
        _topk_kernel,
        grid=(t // tt,),
        in_specs=[pl.BlockSpec((tt, qw), lambda i: (i, 0)),
                  pl.BlockSpec(keys.shape, lambda i: (0, 0, 0))],
        out_specs=[big, big, big, big],
        out_shape=[jax.ShapeDtypeStruct((nrow, t), BF16), jax.ShapeDtypeStruct((nrow, t), BF16),
                   jax.ShapeDtypeStruct((nrow, t), F32), jax.ShapeDtypeStruct((nrow, t), F32)],
        scratch_shapes=[pltpu.VMEM((2, PEER_TOPK, tt), F32), pltpu.VMEM((80, tt), F32),
                        pltpu.VMEM((PEER_TOPK, tt), F32)],
        compiler_params=_params(("arbitrary",)),
        name="peer_topk",
    )(q, keys)


DENSE_ROWS = 16
DENSE_LANES = 256


def _dense_kernel(h2_ref, u_ref, vt_ref, r2_ref, e2_ref, cnt_ref, e1_ref, x1_ref,
                  mod_ref, o_ref, yt_scr, p_scr, wa_scr):
    j = pl.program_id(1)
    ec, tm = p_scr.shape
    n_a = ec // PEER_NKEYS
    n_c = tm // DENSE_LANES
    n_jt = PEER_NKEYS // DENSE_ROWS

    @pl.when(j == 0)
    def _():
        yt_scr[...] = jnp.zeros_like(yt_scr)

    p_scr[...] = lax.dot_general(u_ref[...], h2_ref[...], _NT, preferred_element_type=F32)

    def tile(it, carry):
        al = it // n_c
        c0 = pl.multiple_of((it % n_c) * DENSE_LANES, DENSE_LANES)
        cols = pl.ds(c0, DENSE_LANES)
        a = j * n_a + al
        zero = jnp.zeros((DENSE_ROWS, DENSE_LANES), BF16)
        accs = [zero for _ in range(n_jt)]
        for h in range(PEER_HEADS):
            arow = pl.ds(h * PEER_NKEYS + a, 1)
            cnt = jnp.broadcast_to(cnt_ref[arow, cols], (DENSE_ROWS, DENSE_LANES)).astype(BF16)
            e1 = jnp.broadcast_to(e1_ref[arow, cols], (DENSE_ROWS, DENSE_LANES)).astype(BF16)
            for jt in range(n_jt):
                jrows = pl.ds(h * PEER_NKEYS + jt * DENSE_ROWS, DENSE_ROWS)
                sel = r2_ref[jrows, cols] < cnt
                accs[jt] = accs[jt] + jnp.where(sel, e2_ref[jrows, cols], zero) * e1
        e0 = pl.multiple_of(al * PEER_NKEYS, PEER_NKEYS)
        for jt in range(n_jt):
            erows = pl.ds(e0 + jt * DENSE_ROWS, DENSE_ROWS)
            p = p_scr[erows, cols]
            act = 0.5 * p * (1.0 + lax.erf(p * INV_SQRT2))
            wa_scr[erows, cols] = accs[jt] * act.astype(BF16)
        return carry

    lax.fori_loop(0, n_a * n_c, tile, 0)

    yt_scr[...] += jnp.dot(vt_ref[...], wa_scr[...], preferred_element_type=F32)

    @pl.when(j == pl.num_programs(1) - 1)
    def _():
        o_ref[...] = x1_ref[...] + mod_ref[5:6, :] * yt_scr[...].T


def _dense(h2, u, vt, r2, e2, cnt, e1, x1, mod, seq, tm, ec):
    t, d = x1.shape
    n_exp = u.shape[0]
    nrow = PEER_HEADS * PEER_NKEYS
    per_batch = seq // tm
    tok = pl.BlockSpec((tm, d), lambda i, j: (i, 0))
    big = pl.BlockSpec((nrow, tm), lambda i, j: (0, i))
    return pl.pallas_call(
        _dense_kernel,
        grid=(t // tm, n_exp // ec),
        in_specs=[tok,
                  pl.BlockSpec((ec, d), lambda i, j: (j, 0)),
                  pl.BlockSpec((d, ec), lambda i, j: (0, j)),
                  big, big, big, big,
                  tok,
                  pl.BlockSpec((None, 6, d), lambda i, j: (i // per_batch, 0, 0))],
        out_specs=tok,
        out_shape=jax.ShapeDtypeStruct((t, d), F32),
        scratch_shapes=[pltpu.VMEM((d, tm), F32), pltpu.VMEM((ec, tm), F32),
                        pltpu.VMEM((ec, tm), BF16)],
        compiler_params=_params(("arbitrary", "arbitrary")),
        name="peer_dense",
    )(h2, u, vt, r2, e2, cnt, e1, x1, mod)


def _tile(n, pref):
    return pref if n % pref == 0 else n


def kernel(x, c, positions, norm1_g, norm2_g, w_ada, b_ada, w_in, q_norm_g, k_norm_g, attn_sink,
           conv_w, pool_w, pool_scale, w_out, peer_wq, peer_keys, peer_u, peer_v):
    batch, seq, d = x.shape
    t = batch * seq
    n_layers = w_ada.shape[0]
    tm = _tile(seq, 512)
    tt = _tile(seq, 256)
    ec = 512

    inv = ROPE_THETA ** (-jnp.arange(0, ROPE_DIM, 2, dtype=F32) / ROPE_DIM)
    lane = jnp.arange(LANES) % HEAD_DIM
    inv_lane = jnp.where(lane < ROPE_DIM, inv[lane % (ROPE_DIM // 2)], 0.0).reshape(1, LANES)
    sgn_lane = jnp.where(lane < ROPE_DIM // 2, -1.0, 1.0).astype(F32).reshape(1, LANES)
    head_id = jnp.arange(ATTN_WIDTH) // HEAD_DIM
    gavg = jnp.where(head_id[:, None] == head_id[None, :], 1.0 / HEAD_DIM, 0.0).astype(BF16)
    pos = positions.reshape(t, 1)

    mod_all = _ada(c, w_ada, b_ada).reshape(n_layers, batch, 6, d)
    xt = x.reshape(t, d)
    for l in range(n_layers):
        mod = mod_all[l]
        qg = jnp.tile(q_norm_g[l], N_Q_HEADS).reshape(1, ATTN_WIDTH)
        kg = jnp.tile(k_norm_g[l], N_KV_HEADS).reshape(1, KV_WIDTH)
        q, kvx, cp = _in_proj(xt, mod, norm1_g[l].reshape(1, d), w_in[l].astype(BF16), qg, kg, gavg,
                              pos, inv_lane, sgn_lane, seq, tm)
        pool_bd = jax.scipy.linalg.block_diag(*[pool_w[l, g] for g in range(pool_w.shape[1])])
        mix = _mix(attn_sink[l], q, kvx, cp, conv_w[l], pool_bd.astype(BF16),
                   pool_scale[l].reshape(1, POOL_WIDTH), batch, seq)
        x1, h2, pq = _out_proj(xt, mix, mod, w_out[l].astype(BF16), norm2_g[l].reshape(1, d),
                               peer_wq[l].astype(BF16), seq, tm)
        r2, e2, cnt, e1 = _topk(pq, peer_keys[l].astype(BF16), tt)
        xt = _dense(h2, peer_u[l].astype(BF16), peer_v[l].T.astype(BF16), r2, e2, cnt, e1,
                    x1, mod, seq, tm, ec)
    return xt.reshape(batch, seq, d)
```
